```python
import math
import jax, jax.numpy as jnp
from jax import lax
import numpy as np

D_MODEL = 1024
BATCH = 8
SEQ = 2048
DEPTH = 2

A_HEADS = 8
A_HEAD_DIM = 64
A_V_DIM = 2 * A_HEAD_DIM
A_WIDTH = A_HEADS * A_V_DIM
ROPE_THETA = 500000.0
ROPE_DIM = A_HEAD_DIM // 4
Q_BLOCK = 128
B_GROUPS = 4
B_GROUP_DIM = 128
B_WIDTH = B_GROUPS * B_GROUP_DIM
C_GROUPS = 4
C_GROUP_DIM = 128
C_WIDTH = C_GROUPS * C_GROUP_DIM
C_CHUNK = 128
N_BRANCH = 3
IN_COLS = 3 * A_WIDTH + B_WIDTH + 2 * C_WIDTH + N_BRANCH * D_MODEL
N_EXPERTS = 16
EXPERT_FF = 2752
EC_FACTOR = 2
EPS = 1e-6

kernel_name = "hybrid_diffattn_fnet_gmlp_ecmoe_encoder"


def rmsnorm(x, g=None):
    xf = x.astype(jnp.float32)
    y = xf * lax.rsqrt(jnp.mean(xf * xf, axis=-1, keepdims=True) + EPS)
    if g is not None:
        y = y * g.astype(jnp.float32)
    return y.astype(x.dtype)


def modulate(h, shift, scale):
    return h * (1 + scale[:, None, :]) + shift[:, None, :]


def lambda_init_fn(layer_idx):
    return 0.8 - 0.6 * math.exp(-0.3 * layer_idx)


def rope_tables(positions):
    inv_freq = ROPE_THETA ** (-jnp.arange(0, ROPE_DIM, 2, dtype=jnp.float32) / ROPE_DIM)
    ang = positions.astype(jnp.float32)[..., None] * inv_freq
    return jnp.cos(ang), jnp.sin(ang)


def apply_partial_rope(t, cos, sin):
    rot, keep = t[..., :ROPE_DIM], t[..., ROPE_DIM:]
    r1, r2 = rot[..., :ROPE_DIM // 2], rot[..., ROPE_DIM // 2:]
    c = cos[:, :, None, None, :].astype(t.dtype)
    s = sin[:, :, None, None, :].astype(t.dtype)
    return jnp.concatenate([r1 * c - r2 * s, r1 * s + r2 * c, keep], axis=-1)


def diff_attention(q, k, v, lam):
    bsz, seq = q.shape[0], q.shape[1]
    n_blk = seq // Q_BLOCK
    scale = A_HEAD_DIM ** -0.5
    q_blocks = q.reshape(bsz, n_blk, Q_BLOCK, A_HEADS, 2, A_HEAD_DIM).transpose(1, 0, 2, 3, 4, 5)

    def one_block(qi):
        s = jnp.einsum('bqhcd,bkhcd->bchqk', qi, k).astype(jnp.float32) * scale
        p = jax.nn.softmax(s, axis=-1)
        p = p[:, 0] - lam * p[:, 1]
        return jnp.einsum('bhqk,bkhe->bqhe', p.astype(v.dtype), v)

    o = lax.map(one_block, q_blocks)
    return o.transpose(1, 0, 2, 3, 4).reshape(bsz, seq, A_HEADS, A_V_DIM)


def fourier_mix(h):
    bsz, seq = h.shape[0], h.shape[1]
    hg = h.reshape(bsz, seq, B_GROUPS, B_GROUP_DIM).astype(jnp.float32)
    f = jnp.fft.fft2(hg, axes=(1, 3), norm='ortho').real
    return f.reshape(bsz, seq, B_WIDTH).astype(h.dtype)


def spatial_gating(z, w_s, b_s):
    bsz, seq = z.shape[0], z.shape[1]
    u, v = jnp.split(z, 2, axis=-1)
    n_chunk = seq // C_CHUNK
    vg = rmsnorm(v.reshape(bsz, n_chunk, C_CHUNK, C_GROUPS, C_GROUP_DIM))
    mixed = jnp.einsum('gpq,bnqgc->bnpgc', w_s.astype(z.dtype), vg) + b_s.T[None, None, :, :, None]
    return u * mixed.reshape(bsz, seq, C_WIDTH)


def expert_choice_ffn(h, w_router, w1, w3, w2):
    bsz, seq, d = h.shape
    cap = EC_FACTOR * seq // N_EXPERTS
    logits = jnp.einsum('bsd,de->bse', h, w_router).astype(jnp.float32)
    aff = jax.nn.softmax(logits, axis=-1)
    g, idx = lax.top_k(jnp.swapaxes(aff, 1, 2), cap)
    xin = jax.vmap(lambda hb, ib: hb[ib])(h, idx)
    hid = jax.nn.silu(jnp.einsum('becd,edf->becf', xin, w1)) * jnp.einsum('becd,edf->becf', xin, w3)
    y = jnp.einsum('becf,efd->becd', hid, w2) * g[..., None].astype(h.dtype)
    flat = (idx + (jnp.arange(bsz, dtype=jnp.int32) * seq)[:, None, None]).reshape(-1)
    out = jnp.zeros((bsz * seq, d), h.dtype).at[flat].add(y.reshape(-1, d))
    return out.reshape(bsz, seq, d)


def setup_inputs(seed: int = 0) -> dict:
    key = jax.random.key(seed)
    ks = jax.random.split(key, 26)
    D, L = D_MODEL, DEPTH

    def nrm(k, shape, fan_in, gain=1.0):
        return jax.random.normal(k, shape, jnp.float32) * (gain * fan_in ** -0.5)

    def near_one(k, shape):
        return 1.0 + 0.01 * jax.random.normal(k, shape, jnp.float32)

    offs = jax.random.randint(ks[2], (BATCH, 1), 0, 4096, dtype=jnp.int32)
    return {
        'x': jax.random.normal(ks[0], (BATCH, SEQ, D), jnp.float32),
        'c': jax.random.normal(ks[1], (BATCH, D), jnp.float32),
        'positions': offs + jnp.arange(SEQ, dtype=jnp.int32)[None, :],
        'w_ada': nrm(ks[3], (L, D, 6 * D), D, 0.5),
        'b_ada': 0.01 * jax.random.normal(ks[4], (L, 6 * D), jnp.float32),
        'g_mix': near_one(ks[5], (L, D)),
        'g_ffn': near_one(ks[6], (L, D)),
        'w_in': nrm(ks[7], (L, D, IN_COLS), D),
        'lam_q1': 0.1 * jax.random.normal(ks[8], (L, A_HEAD_DIM), jnp.float32),
        'lam_k1': 0.1 * jax.random.normal(ks[9], (L, A_HEAD_DIM), jnp.float32),
        'lam_q2': 0.1 * jax.random.normal(ks[10], (L, A_HEAD_DIM), jnp.float32),
        'lam_k2': 0.1 * jax.random.normal(ks[11], (L, A_HEAD_DIM), jnp.float32),
        'a_norm_g': near_one(ks[12], (L, A_V_DIM)),
        'w_s': nrm(ks[13], (L, C_GROUPS, C_CHUNK, C_CHUNK), C_CHUNK),
        'b_s': near_one(ks[14], (L, C_GROUPS, C_CHUNK)),
        'p_a': nrm(ks[15], (L, A_WIDTH, D), A_WIDTH),
        'p_b': nrm(ks[16], (L, B_WIDTH, D), B_WIDTH),
        'p_c': nrm(ks[17], (L, C_WIDTH, D), C_WIDTH),
        'w_out': nrm(ks[18], (L, D, D), D),
        'w_router': nrm(ks[19], (L, D, N_EXPERTS), D),
        'w1': nrm(ks[20], (L, N_EXPERTS, D, EXPERT_FF), D),
        'w3': nrm(ks[21], (L, N_EXPERTS, D, EXPERT_FF), D),
        'w2': nrm(ks[22], (L, N_EXPERTS, EXPERT_FF, D), EXPERT_FF),
        'g_final': near_one(ks[23], (D,)),
    }


def reference(x, c, positions, w_ada, b_ada, g_mix, g_ffn, w_in, lam_q1, lam_k1, lam_q2, lam_k2,
              a_norm_g, w_s, b_s, p_a, p_b, p_c, w_out, w_router, w1, w3, w2, g_final):
    bsz, seq, d = x.shape
    cos, sin = rope_tables(positions)
    c_act = jax.nn.silu(c)
    splits = [A_WIDTH, 2 * A_WIDTH, 3 * A_WIDTH, 3 * A_WIDTH + B_WIDTH,
              3 * A_WIDTH + B_WIDTH + 2 * C_WIDTH]
    for l in range(DEPTH):
        ada = c_act @ w_ada[l] + b_ada[l]
        sh1, sc1, gt1, sh2, sc2, gt2 = jnp.split(ada, 6, axis=-1)

        h = modulate(rmsnorm(x, g_mix[l]), sh1, sc1)
        proj = h @ w_in[l]
        q, k, v, fb, zc, gates = jnp.split(proj, splits, axis=-1)

        q = apply_partial_rope(q.reshape(bsz, seq, A_HEADS, 2, A_HEAD_DIM), cos, sin)
        k = apply_partial_rope(k.reshape(bsz, seq, A_HEADS, 2, A_HEAD_DIM), cos, sin)
        v = v.reshape(bsz, seq, A_HEADS, A_V_DIM)
        lam_init = lambda_init_fn(l)
        lam = (jnp.exp(jnp.sum(lam_q1[l].astype(jnp.float32) * lam_k1[l].astype(jnp.float32)))
               - jnp.exp(jnp.sum(lam_q2[l].astype(jnp.float32) * lam_k2[l].astype(jnp.float32)))
               + lam_init)
        a_out = diff_attention(q, k, v, lam)
        a_out = (rmsnorm(a_out, a_norm_g[l]) * (1 - lam_init)).reshape(bsz, seq, A_WIDTH)

        b_out = fourier_mix(fb)

        c_out = spatial_gating(jax.nn.gelu(zc), w_s[l], b_s[l])

        g_a, g_b, g_c = jnp.split(jax.nn.sigmoid(gates), N_BRANCH, axis=-1)
        merged = g_a * (a_out @ p_a[l]) + g_b * (b_out @ p_b[l]) + g_c * (c_out @ p_c[l])
        x = x + gt1[:, None, :] * (merged @ w_out[l])

        h = modulate(rmsnorm(x, g_ffn[l]), sh2, sc2)
        x = x + gt2[:, None, :] * expert_choice_ffn(h, w_router[l], w1[l], w3[l], w2[l])

    return rmsnorm(x, g_final)
```

```python
import functools
import math

import numpy as np
import jax
import jax.numpy as jnp
from jax import lax
from jax.experimental import pallas as pl
from jax.experimental.pallas import tpu as pltpu

D_MODEL = 1024
A_HEADS = 8
A_HEAD_DIM = 64
A_V_DIM = 2 * A_HEAD_DIM
A_WIDTH = A_HEADS * A_V_DIM
ROPE_THETA = 500000.0
ROPE_DIM = A_HEAD_DIM // 4
B_GROUPS = 4
B_GROUP_DIM = 128
B_WIDTH = B_GROUPS * B_GROUP_DIM
C_GROUPS = 4
C_GROUP_DIM = 128
C_WIDTH = C_GROUPS * C_GROUP_DIM
C_CHUNK = 128
N_BRANCH = 3
GATE_COLS = N_BRANCH * D_MODEL
IN_COLS = 3 * A_WIDTH + B_WIDTH + 2 * C_WIDTH + GATE_COLS
N_EXPERTS = 16
EC_FACTOR = 2
EPS = 1e-6

LANE = 128
BF16_SUBLANES = 16
VMEM_LIMIT = 56 * 1024 * 1024

COL_GATES = 0
COL_Q = GATE_COLS
COL_K = COL_Q + A_WIDTH
COL_V = COL_K + A_WIDTH
COL_FB = COL_V + A_WIDTH
COL_ZU = COL_FB + B_WIDTH
COL_ZV = COL_ZU + C_WIDTH
PROJ_TN = 512

F32 = jnp.float32
BF16 = jnp.bfloat16


def _params(*sem):
    return pltpu.CompilerParams(dimension_semantics=sem, vmem_limit_bytes=VMEM_LIMIT)


def _dot(a, b):
    return jnp.dot(a, b, preferred_element_type=F32)


def _dot_nt(a, b):
    return lax.dot_general(a, b, (((1,), (1,)), ((), ())), preferred_element_type=F32)


def _onehot(mask):
    return jnp.where(mask, 1.0, 0.0).astype(BF16)


def _split3(a):
    hi = a.astype(BF16)
    r1 = a - hi.astype(F32)
    mid = r1.astype(BF16)
    lo = (r1 - mid.astype(F32)).astype(BF16)
    return hi, mid, lo


def _dot_nt_f32(a, b):
    a0, a1, a2 = _split3(a)
    b0, b1, b2 = _split3(b)
    out = _dot_nt(a0, b2) + _dot_nt(a1, b1) + _dot_nt(a2, b0)
    out = out + _dot_nt(a0, b1) + _dot_nt(a1, b0)
    return out + _dot_nt(a0, b0)


def _ada_kernel(c_ref, w_ref, b_ref, o_ref):
    cf = c_ref[...]
    ca = cf * (1.0 / (1.0 + jnp.exp(-cf)))
    w = w_ref[...]
    c0, c1, c2 = _split3(ca)
    w0, w1, w2 = _split3(w)
    out = _dot(c0, w2) + _dot(c1, w1) + _dot(c2, w0)
    out = out + _dot(c0, w1) + _dot(c1, w0)
    out = out + _dot(c0, w0)
    o_ref[...] = out + b_ref[...]


def _ada(c, w_ada, b_ada):
    depth, d, n = w_ada.shape
    bsz = c.shape[0]
    tn = 512
    return pl.pallas_call(
        _ada_kernel,
        grid=(depth, n // tn),
        in_specs=[
            pl.BlockSpec((bsz, d), lambda l, j: (0, 0)),
            pl.BlockSpec((None, d, tn), lambda l, j: (l, 0, j)),
            pl.BlockSpec((None, 1, tn), lambda l, j: (l, 0, j)),
        ],
        out_specs=pl.BlockSpec((None, bsz, tn), lambda l, j: (l, 0, j)),
        out_shape=jax.ShapeDtypeStruct((depth, bsz, n), F32),
        compiler_params=_params("arbitrary", "arbitrary"),
        name="ada",
    )(c, w_ada, b_ada.reshape(depth, 1, n))


def _gelu_tanh(x):
    return 0.5 * x * (1.0 + jnp.tanh(math.sqrt(2.0 / math.pi) * (x + 0.044715 * (x * x * x))))


def _inproj_kernel(x_ref, g_ref, sh_ref, sc_ref, w_ref, rc_ref, rs1_ref, rs2_ref, o_ref, h_scr):
    j = pl.program_id(1)

    @pl.when(j == 0)
    def _():
        xf = x_ref[...]
        ms = jnp.mean(xf * xf, axis=-1, keepdims=True)
        y = xf * lax.rsqrt(ms + EPS) * g_ref[...]
        h_scr[...] = (y * (1.0 + sc_ref[...]) + sh_ref[...]).astype(BF16)

    acc = _dot(h_scr[...], w_ref[...])
    j_q = COL_Q // PROJ_TN
    j_k = COL_K // PROJ_TN
    j_v = COL_V // PROJ_TN
    j_zu = COL_ZU // PROJ_TN

    @pl.when(j < j_q)
    def _():
        o_ref[...] = (1.0 / (1.0 + jnp.exp(-acc))).astype(o_ref.dtype)

    def rope(scale):
        rc, rs1, rs2 = rc_ref[...], rs1_ref[...], rs2_ref[...]
        for gi in range(PROJ_TN // LANE):
            t = acc[:, gi * LANE:(gi + 1) * LANE]
            r = t * rc + pltpu.roll(t, LANE - ROPE_DIM // 2, 1) * rs1 + pltpu.roll(t, ROPE_DIM // 2, 1) * rs2
            if scale != 1.0:
                r = r * scale
            o_ref[:, gi * LANE:(gi + 1) * LANE] = r.astype(o_ref.dtype)

    @pl.when((j >= j_q) & (j < j_k))
    def _():
        rope(A_HEAD_DIM ** -0.5)

    @pl.when((j >= j_k) & (j < j_v))
    def _():
        rope(1.0)

    @pl.when((j >= j_v) & (j < j_zu))
    def _():
        o_ref[...] = acc.astype(o_ref.dtype)

    @pl.when(j >= j_zu)
    def _():
        o_ref[...] = _gelu_tanh(acc).astype(o_ref.dtype)


def _inproj(x2, g, sh, sc, w_bf, rc, rs1, rs2, seq, tm):
    rows, d = x2.shape
    n = w_bf.shape[1]
    per_b = seq // tm
    return pl.pallas_call(
        _inproj_kernel,
        grid=(rows // tm, n // PROJ_TN),
        in_specs=[
            pl.BlockSpec((tm, d), lambda i, j: (i, 0)),
            pl.BlockSpec((1, d), lambda i, j: (0, 0)),
            pl.BlockSpec((None, 1, d), lambda i, j: (i // per_b, 0, 0)),
            pl.BlockSpec((None, 1, d), lambda i, j: (i // per_b, 0, 0)),
            pl.BlockSpec((d, PROJ_TN), lambda i, j: (0, j)),
            pl.BlockSpec((tm, LANE), lambda i, j: (i, 0)),
            pl.BlockSpec((tm, LANE), lambda i, j: (i, 0)),
            pl.BlockSpec((tm, LANE), lambda i, j: (i, 0)),
        ],
        out_specs=pl.BlockSpec((tm, PROJ_TN), lambda i, j: (i, j)),
        out_shape=jax.ShapeDtypeStruct((rows, n), BF16),
        scratch_shapes=[pltpu.VMEM((tm, d), BF16)],
        compiler_params=_params("arbitrary", "arbitrary"),
        name="inproj",
    )(x2, g, sh, sc, w_bf, rc, rs1, rs2)


def _attn_kernel(lam_init, lq1_ref, lk1_ref, lq2_ref, lk2_ref, ng_ref, q_ref, k_ref, v_ref, o_ref, vt_scr):
    qi = pl.program_id(2)
    seq = k_ref.shape[0]

    @pl.when(qi == 0)
    def _():
        vt = v_ref[...].astype(F32).T
        vt_scr[0:A_V_DIM, :] = vt.astype(BF16)
        vt_scr[A_V_DIM:, :] = jnp.ones((vt_scr.shape[0] - A_V_DIM, seq), BF16)

    lam = (jnp.exp(jnp.sum(lq1_ref[...] * lk1_ref[...], axis=-1, keepdims=True))
           - jnp.exp(jnp.sum(lq2_ref[...] * lk2_ref[...], axis=-1, keepdims=True)) + lam_init)

    q = q_ref[...]
    k = k_ref[...]
    lane = lax.broadcasted_iota(jnp.int32, q.shape, 1)
    outs = []
    for comp in range(2):
        qm = q * _onehot((lane >= comp * A_HEAD_DIM) & (lane < (comp + 1) * A_HEAD_DIM))
        st = _dot_nt(k, qm)
        m = jnp.max(st, axis=0, keepdims=True)
        p = jnp.exp(st - m).astype(BF16)
        ot = _dot(vt_scr[...], p)
        outs.append(ot[0:A_V_DIM, :] / ot[A_V_DIM:A_V_DIM + 1, :])
    o = outs[0] - lam * outs[1]
    ms = jnp.mean(o * o, axis=0, keepdims=True)
    o = o * lax.rsqrt(ms + EPS)
    o = o.T * (ng_ref[...] * (1.0 - lam_init))
    o_ref[...] = o.astype(o_ref.dtype)


def _attention(proj, lq1, lk1, lq2, lk2, ng, lam_init, bsz, seq, tq):
    rows = proj.shape[0]
    nq = seq // tq
    qb, kb, vb = COL_Q // LANE, COL_K // LANE, COL_V // LANE
    small = pl.BlockSpec((1, A_HEAD_DIM), lambda b, h, i: (0, 0))
    return pl.pallas_call(
        functools.partial(_attn_kernel, lam_init),
        grid=(bsz, A_HEADS, nq),
        in_specs=[
            small, small, small, small,
            pl.BlockSpec((1, A_V_DIM), lambda b, h, i: (0, 0)),
            pl.BlockSpec((tq, LANE), lambda b, h, i: (b * nq + i, qb + h)),
            pl.BlockSpec((seq, LANE), lambda b, h, i: (b, kb + h)),
            pl.BlockSpec((seq, LANE), lambda b, h, i: (b, vb + h)),
        ],
        out_specs=pl.BlockSpec((tq, LANE), lambda b, h, i: (b * nq + i, h)),
        out_shape=jax.ShapeDtypeStruct((rows, A_WIDTH), BF16),
        scratch_shapes=[pltpu.VMEM((A_V_DIM + BF16_SUBLANES, seq), BF16)],
        compiler_params=_params("arbitrary", "arbitrary", "arbitrary"),
        name="diff_attn",
    )(lq1, lk1, lq2, lk2, ng, proj, proj, proj)


def _dft_mats(seq):
    n = np.arange(seq, dtype=np.int64)
    ang = 2.0 * np.pi * ((n[:, None] * n[None, :]) % seq).astype(np.float64) / seq
    big = np.concatenate([np.cos(ang), -np.sin(ang)], axis=1) / math.sqrt(seq)
    c = np.arange(B_GROUP_DIM, dtype=np.int64)
    angc = 2.0 * np.pi * ((c[:, None] * c[None, :]) % B_GROUP_DIM).astype(np.float64) / B_GROUP_DIM
    small = np.concatenate([np.cos(angc), np.sin(angc)], axis=1) / math.sqrt(B_GROUP_DIM)
    return jnp.asarray(big, dtype=F32).astype(BF16), jnp.asarray(small, dtype=F32).astype(BF16)


def _fnet_kernel(x_ref, wc_ref, d_ref, o_ref, y_scr):
    seq = x_ref.shape[0]

    @pl.when(pl.program_id(1) == 0)
    def _():
        for gi in range(B_GROUPS):
            xg = x_ref[:, gi * B_GROUP_DIM:(gi + 1) * B_GROUP_DIM]
            yz = _dot(xg, wc_ref[...])
            y_scr[0:seq, gi * B_GROUP_DIM:(gi + 1) * B_GROUP_DIM] = yz[:, :B_GROUP_DIM].astype(BF16)
            y_scr[seq:2 * seq, gi * B_GROUP_DIM:(gi + 1) * B_GROUP_DIM] = yz[:, B_GROUP_DIM:].astype(BF16)

    o_ref[...] = _dot(d_ref[...], y_scr[...]).astype(o_ref.dtype)


def _fnet(proj, dbig, dsmall, bsz, seq, tm):
    rows = proj.shape[0]
    nt = seq // tm
    return pl.pallas_call(
        _fnet_kernel,
        grid=(bsz, nt),
        in_specs=[
            pl.BlockSpec((seq, B_WIDTH), lambda b, i: (b, COL_FB // B_WIDTH)),
            pl.BlockSpec((B_GROUP_DIM, 2 * B_GROUP_DIM), lambda b, i: (0, 0)),
            pl.BlockSpec((tm, 2 * seq), lambda b, i: (i, 0)),
        ],
        out_specs=pl.BlockSpec((tm, B_WIDTH), lambda b, i: (b * nt + i, 0)),
        out_shape=jax.ShapeDtypeStruct((rows, B_WIDTH), BF16),
        scratch_shapes=[pltpu.VMEM((2 * seq, B_WIDTH), BF16)],
        compiler_params=_params("arbitrary", "arbitrary"),
        name="fnet",
    )(proj, dsmall, dbig)


def _sgu_kernel(u_ref, v_ref, ws_ref, bs_ref, o_ref):
    tm = u_ref.shape[0]
    for n in range(tm // C_CHUNK):
        r0 = n * C_CHUNK
        for gi in range(C_GROUPS):
            c0 = gi * C_GROUP_DIM
            vb = v_ref[r0:r0 + C_CHUNK, c0:c0 + C_GROUP_DIM].astype(F32)
            vg = vb * lax.rsqrt(jnp.mean(vb * vb, axis=-1, keepdims=True) + EPS)
            mixed = _dot(ws_ref[gi], vg.astype(BF16)) + bs_ref[gi]
            ub = u_ref[r0:r0 + C_CHUNK, c0:c0 + C_GROUP_DIM].astype(F32)
            o_ref[r0:r0 + C_CHUNK, c0:c0 + C_GROUP_DIM] = (ub * mixed).astype(o_ref.dtype)


def _sgu(proj, ws_bf, bs_col, tm):
    rows = proj.shape[0]
    return pl.pallas_call(
        _sgu_kernel,
        grid=(rows // tm,),
        in_specs=[
            pl.BlockSpec((tm, C_WIDTH), lambda i: (i, COL_ZU // C_WIDTH)),
            pl.BlockSpec((tm, C_WIDTH), lambda i: (i, COL_ZV // C_WIDTH)),
            pl.BlockSpec((C_GROUPS, C_CHUNK, C_CHUNK), lambda i: (0, 0, 0)),
            pl.BlockSpec((C_GROUPS, C_CHUNK, 1), lambda i: (0, 0, 0)),
        ],
        out_specs=pl.BlockSpec((tm, C_WIDTH), lambda i: (i, 0)),
        out_shape=jax.ShapeDtypeStruct((rows, C_WIDTH), BF16),
        compiler_params=_params("arbitrary"),
        name="sgu",
    )(proj, proj, ws_bf, bs_col)


def _merge_kernel(a_ref, b_ref, c_ref, gates_ref, x_ref, pa_ref, pb_ref, pc_ref, wo_ref,
                  gt1_ref, g2_ref, sh2_ref, sc2_ref, wr_ref, x1_ref, h2_ref, aff_ref):
    d = x_ref.shape[1]
    ga = gates_ref[:, 0:d].astype(F32)
    gb = gates_ref[:, d:2 * d].astype(F32)
    gc = gates_ref[:, 2 * d:3 * d].astype(F32)
    merged = ga * _dot(a_ref[...], pa_ref[...])
    merged = merged + gb * _dot(b_ref[...], pb_ref[...])
    merged = merged + gc * _dot(c_ref[...], pc_ref[...])
    x1 = x_ref[...] + gt1_ref[...] * _dot(merged.astype(BF16), wo_ref[...])
    x1_ref[...] = x1
    ms = jnp.mean(x1 * x1, axis=-1, keepdims=True)
    h = x1 * lax.rsqrt(ms + EPS) * g2_ref[...]
    h = h * (1.0 + sc2_ref[...]) + sh2_ref[...]
    h2_ref[...] = h.astype(h2_ref.dtype)
    logits = _dot_nt_f32(wr_ref[...], h)
    mx = jnp.max(logits, axis=0, keepdims=True)
    ex = jnp.exp(logits - mx)
    aff_ref[...] = ex / jnp.sum(ex, axis=0, keepdims=True)


def _merge(a_out, b_out, c_out, proj, x2, pa, pb, pc, wo, gt1, g2, sh2, sc2, wr_t, seq, tm):
    rows, d = x2.shape
    per_b = seq // tm
    ne = wr_t.shape[0]
    full = lambda shape: pl.BlockSpec(shape, lambda i: (0,) * len(shape))
    bvec = pl.BlockSpec((None, 1, d), lambda i: (i // per_b, 0, 0))
    return pl.pallas_call(
        _merge_kernel,
        grid=(rows // tm,),
        in_specs=[
            pl.BlockSpec((tm, A_WIDTH), lambda i: (i, 0)),
            pl.BlockSpec((tm, B_WIDTH), lambda i: (i, 0)),
            pl.BlockSpec((tm, C_WIDTH), lambda i: (i, 0)),
            pl.BlockSpec((tm, GATE_COLS), lambda i: (i, 0)),
            pl.BlockSpec((tm, d), lambda i: (i, 0)),
            full(pa.shape), full(pb.shape), full(pc.shape), full(wo.shape),
            bvec, full((1, d)), bvec, bvec, full(wr_t.shape),
        ],
        out_specs=[
            pl.BlockSpec((tm, d), lambda i: (i, 0)),
            pl.BlockSpec((tm, d), lambda i: (i, 0)),
            pl.BlockSpec((ne, tm), lambda i: (0, i)),
        ],
        out_shape=[
            jax.ShapeDtypeStruct((rows, d), F32),
            jax.ShapeDtypeStruct((rows, d), BF16),
            jax.ShapeDtypeStruct((ne, rows), F32),
        ],
        compiler_params=_params("arbitrary"),
        name="merge",
    )(a_out, b_out, c_out, proj, x2, pa, pb, pc, wo, gt1, g2, sh2, sc2, wr_t)


def _route_kernel(cap, aff_ref, pos_ref, tri_scr):
    seq = aff_ref.shape[1]

    @pl.when(pl.program_id(0) == 0)
    def _():
        chunk = 256
        r = lax.broadcasted_iota(jnp.int32, (chunk, seq), 0)
        c = lax.broadcasted_iota(jnp.int32, (chunk, seq), 1)
        for r0 in range(0, seq, chunk):
            tri_scr[r0:r0 + chunk, :] = _onehot(r + r0 < c)

    aff = aff_ref[...]
    one = jnp.ones(aff.shape, jnp.int32)
    zero = jnp.zeros(aff.shape, jnp.int32)
    thr_bits = jnp.zeros((aff.shape[0], 1), jnp.int32)
    for bit in range(30, -1, -1):
        cand = thr_bits | (1 << bit)
        cnt = jnp.sum(jnp.where(aff >= pltpu.bitcast(cand, F32), one, zero), axis=1, keepdims=True)
        thr_bits = jnp.where(cnt >= cap, cand, thr_bits)
    thr = pltpu.bitcast(thr_bits, F32)
    gt = aff > thr
    eq = aff == thr
    need = cap - jnp.sum(jnp.where(gt, one, zero), axis=1, keepdims=True)
    eq_before = _dot(_onehot(eq), tri_scr[...])
    sel = gt | (eq & (eq_before < need.astype(F32)))
    slot = _dot(_onehot(sel), tri_scr[...])
    pos_ref[...] = jnp.where(sel, slot.astype(jnp.int32), -one)


def _route(aff_t, bsz, seq, cap):
    ne = aff_t.shape[0]
    return pl.pallas_call(
        functools.partial(_route_kernel, cap),
        grid=(bsz,),
        in_specs=[pl.BlockSpec((ne, seq), lambda b: (0, b))],
        out_specs=pl.BlockSpec((None, ne, seq), lambda b: (b, 0, 0)),
        out_shape=jax.ShapeDtypeStruct((bsz, ne, seq), jnp.int32),
        scratch_shapes=[pltpu.VMEM((seq, seq), BF16)],
        compiler_params=_params("arbitrary"),
        name="route",
    )(aff_t)


def _gather_kernel(cap, pos_ref, aff_ref, h_ref, xin_ref, g_ref):
    seq = h_ref.shape[0]
    slot = lax.broadcasted_iota(jnp.int32, (cap, seq), 0)
    sel = pos_ref[...] == slot
    xin_ref[...] = _dot(_onehot(sel), h_ref[...]).astype(xin_ref.dtype)
    g_ref[...] = jnp.sum(jnp.where(sel, aff_ref[...], 0.0), axis=1, keepdims=True)


def _gather(pos4, aff4, h2, bsz, seq, cap):
    ne = pos4.shape[1]
    d = h2.shape[1]
    return pl.pallas_call(
        functools.partial(_gather_kernel, cap),
        grid=(bsz, ne),
        in_specs=[
            pl.BlockSpec((None, None, 1, seq), lambda b, e: (b, e, 0, 0)),
            pl.BlockSpec((None, None, 1, seq), lambda b, e: (e, b, 0, 0)),
            pl.BlockSpec((seq, d), lambda b, e: (b, 0)),
        ],
        out_specs=[
            pl.BlockSpec((None, None, cap, d), lambda b, e: (e, b, 0, 0)),
            pl.BlockSpec((None, None, cap, 1), lambda b, e: (e, b, 0, 0)),
        ],
        out_shape=[
            jax.ShapeDtypeStruct((ne, bsz, cap, d), BF16),
            jax.ShapeDtypeStruct((ne, bsz, cap, 1), F32),
        ],
        compiler_params=_params("arbitrary", "arbitrary"),
        name="moe_gather",
    )(pos4, aff4, h2)


def _expert_kernel(x_ref, w1_ref, w3_ref, w2_ref, g_ref, y_ref, acc_scr):
    f = pl.program_id(1)
    xb = x_ref[...]
    a = _dot(xb, w1_ref[...])
    b = _dot(xb, w3_ref[...])
    hid = (a * (1.0 / (1.0 + jnp.exp(-a))) * b).astype(BF16)
    part = _dot(hid, w2_ref[...])

    @pl.when(f == 0)
    def _():
        acc_scr[...] = part

    @pl.when(f > 0)
    def _():
        acc_scr[...] += part

    @pl.when(f == pl.num_programs(1) - 1)
    def _():
        y_ref[...] = (acc_scr[...] * g_ref[...]).astype(y_ref.dtype)


def _experts(xin, w1, w3, w2, g, tf):
    ne, m, d = xin.shape
    fp = w1.shape[2]
    return pl.pallas_call(
        _expert_kernel,
        grid=(ne, fp // tf),
        in_specs=[
            pl.BlockSpec((None, m, d), lambda e, f: (e, 0, 0)),
            pl.BlockSpec((None, d, tf), lambda e, f: (e, 0, f)),
            pl.BlockSpec((None, d, tf), lambda e, f: (e, 0, f)),
            pl.BlockSpec((None, tf, d), lambda e, f: (e, f, 0)),
            pl.BlockSpec((None, m, 1), lambda e, f: (e, 0, 0)),
        ],
        out_specs=pl.BlockSpec((None, m, d), lambda e, f: (e, 0, 0)),
        out_shape=jax.ShapeDtypeStruct((ne, m, d), BF16),
        scratch_shapes=[pltpu.VMEM((m, d), F32)],
        compiler_params=_params("arbitrary", "arbitrary"),
        name="experts",
    )(xin, w1, w3, w2, g)


def _scatter_kernel(cap, final, pos_ref, y_ref, x_ref, gt_ref, gf_ref, o_ref):
    tm = x_ref.shape[0]
    ne = y_ref.shape[0]
    slot = lax.broadcasted_iota(jnp.int32, (tm, cap), 1)
    pos = pos_ref[...]
    acc = jnp.zeros(x_ref.shape, F32)
    for e in range(ne):
        acc = acc + _dot(_onehot(pos[:, e:e + 1] == slot), y_ref[e])
    xn = x_ref[...] + gt_ref[...] * acc
    if final:
        ms = jnp.mean(xn * xn, axis=-1, keepdims=True)
        xn = xn * lax.rsqrt(ms + EPS) * gf_ref[...]
    o_ref[...] = xn


def _scatter(pos_t, y4, x1, gt2, gf, seq, cap, tm, final):
    rows, d = x1.shape
    ne = y4.shape[0]
    per_b = seq // tm
    return pl.pallas_call(
        functools.partial(_scatter_kernel, cap, final),
        grid=(rows // tm,),
        in_specs=[
            pl.BlockSpec((tm, ne), lambda i: (i, 0)),
            pl.BlockSpec((ne, None, cap, d), lambda i: (0, i // per_b, 0, 0)),
            pl.BlockSpec((tm, d), lambda i: (i, 0)),
            pl.BlockSpec((None, 1, d), lambda i: (i // per_b, 0, 0)),
            pl.BlockSpec((1, d), lambda i: (0, 0)),
        ],
        out_specs=pl.BlockSpec((tm, d), lambda i: (i, 0)),
        out_shape=jax.ShapeDtypeStruct((rows, d), F32),
        compiler_params=_params("arbitrary"),
        name="moe_scatter",
    )(pos_t, y4, x1, gt2, gf)


def _rope_lane_tables(positions):
    half = ROPE_DIM // 2
    inv_freq = ROPE_THETA ** (-jnp.arange(0, ROPE_DIM, 2, dtype=F32) / ROPE_DIM)
    ang = positions.astype(F32)[..., None] * inv_freq
    cos, sin = jnp.cos(ang), jnp.sin(ang)
    rows = cos.shape[0] * cos.shape[1]
    cos = cos.reshape(rows, half)
    sin = sin.reshape(rows, half)
    pad = jnp.zeros((rows, A_HEAD_DIM - ROPE_DIM), F32)
    zer = jnp.zeros((rows, half), F32)
    rc = jnp.concatenate([cos, cos, pad + 1.0], axis=1)
    rs1 = jnp.concatenate([-sin, zer, pad], axis=1)
    rs2 = jnp.concatenate([zer, sin, pad], axis=1)
    tile = lambda t: jnp.concatenate([t] * (LANE // A_HEAD_DIM), axis=1)
    return tile(rc), tile(rs1), tile(rs2)


def _lambda_init(layer_idx):
    return 0.8 - 0.6 * math.exp(-0.3 * layer_idx)


def kernel(x, c, positions, w_ada, b_ada, g_mix, g_ffn, w_in, lam_q1, lam_k1, lam_q2, lam_k2,
           a_norm_g, w_s, b_s, p_a, p_b, p_c, w_out, w_router, w1, w3, w2, g_final):
    bsz, seq, d = x.shape
    depth = w_in.shape[0]
    rows = bsz * seq
    cap = EC_FACTOR * seq // N_EXPERTS
    ff = w1.shape[-1]
    tf = 256
    ff_pad = -(-ff // tf) * tf

    rc, rs1, rs2 = _rope_lane_tables(positions)
    dbig, dsmall = _dft_mats(seq)
    ada = _ada(c, w_ada, b_ada)
    x2 = x.reshape(rows, d)

    a3 = 3 * A_WIDTH
    nb = a3 + B_WIDTH + 2 * C_WIDTH
    w_in_p = jnp.concatenate([w_in[:, :, nb:], w_in[:, :, :nb]], axis=2).astype(BF16)
    pad_f = ((0, 0), (0, 0), (0, 0), (0, ff_pad - ff))
    w1_p = jnp.pad(w1.astype(BF16), pad_f)
    w3_p = jnp.pad(w3.astype(BF16), pad_f)
    w2_p = jnp.pad(w2.astype(BF16), ((0, 0), (0, 0), (0, ff_pad - ff), (0, 0)))

    for l in range(depth):
        mod = [ada[l, :, i * d:(i + 1) * d].reshape(bsz, 1, d) for i in range(6)]
        sh1, sc1, gt1, sh2, sc2, gt2 = mod

        proj = _inproj(x2, g_mix[l].reshape(1, d), sh1, sc1, w_in_p[l], rc, rs1, rs2, seq, tm=1024)
        a_out = _attention(proj, lam_q1[l].reshape(1, -1), lam_k1[l].reshape(1, -1),
                           lam_q2[l].reshape(1, -1), lam_k2[l].reshape(1, -1),
                           a_norm_g[l].reshape(1, -1), _lambda_init(l), bsz, seq, tq=256)
        b_out = _fnet(proj, dbig, dsmall, bsz, seq, tm=512)
        c_out = _sgu(proj, w_s[l].astype(BF16), b_s[l].reshape(C_GROUPS, C_CHUNK, 1), tm=1024)
        x1, h2, aff_t = _merge(a_out, b_out, c_out, proj, x2,
                               p_a[l].astype(BF16), p_b[l].astype(BF16), p_c[l].astype(BF16),
                               w_out[l].astype(BF16), gt1, g_ffn[l].reshape(1, d), sh2, sc2,
                               w_router[l].T, seq, tm=512)

        pos = _route(aff_t, bsz, seq, cap)
        xin, gsel = _gather(pos.reshape(bsz, N_EXPERTS, 1, seq),
                            aff_t.reshape(N_EXPERTS, bsz, 1, seq), h2, bsz, seq, cap)
        y = _experts(xin.reshape(N_EXPERTS, bsz * cap, d), w1_p[l], w3_p[l], w2_p[l],
                     gsel.reshape(N_EXPERTS, bsz * cap, 1), tf)
        pos_t = jnp.swapaxes(pos, 1, 2).reshape(rows, N_EXPERTS)
        x2 = _scatter(pos_t, y.reshape(N_EXPERTS, bsz, cap, d), x1, gt2, g_final.reshape(1, d),
                      seq, cap, tm=512, final=(l == depth - 1))

    return x2.reshape(bsz, seq, d)
```

```python
import functools
import math

import numpy as np
import jax
import jax.numpy as jnp
from jax import lax
from jax.experimental import pallas as pl
from jax.experimental.pallas import tpu as pltpu

D_MODEL = 1024
A_HEADS = 8
A_HEAD_DIM = 64
A_V_DIM = 2 * A_HEAD_DIM
A_WIDTH = A_HEADS * A_V_DIM
ROPE_THETA = 500000.0
ROPE_DIM = A_HEAD_DIM // 4
B_GROUPS = 4
B_GROUP_DIM = 128
B_WIDTH = B_GROUPS * B_GROUP_DIM
C_GROUPS = 4
C_GROUP_DIM = 128
C_WIDTH = C_GROUPS * C_GROUP_DIM
C_CHUNK = 128
N_BRANCH = 3
GATE_COLS = N_BRANCH * D_MODEL
IN_COLS = 3 * A_WIDTH + B_WIDTH + 2 * C_WIDTH + GATE_COLS
N_EXPERTS = 16
EC_FACTOR = 2
EPS = 1e-6
Q_SCALE = A_HEAD_DIM ** -0.5 * math.log2(math.e)

LANE = 128
BF16_SUBLANES = 16
VMEM_LIMIT = 56 * 1024 * 1024

WORK_Q = GATE_COLS
WORK_K = WORK_Q + A_WIDTH
WORK_V = WORK_K + A_WIDTH
WORK_FB = WORK_V + A_WIDTH
WORK_ZU = WORK_FB + B_WIDTH
COL_FB = GATE_COLS
COL_ZU = COL_FB + B_WIDTH
COL_ZV = COL_ZU + C_WIDTH
PROJ_COLS = COL_ZV + C_WIDTH
PROJ_TN = 512

F32 = jnp.float32
BF16 = jnp.bfloat16


def _params(*sem, flags=None):
    return pltpu.CompilerParams(dimension_semantics=sem, vmem_limit_bytes=VMEM_LIMIT, flags=flags)


def _dot(a, b):
    return jnp.dot(a, b, preferred_element_type=F32)


def _dot_nt(a, b):
    return lax.dot_general(a, b, (((1,), (1,)), ((), ())), preferred_element_type=F32)


def _onehot(mask):
    return jnp.where(mask, 1.0, 0.0).astype(BF16)


def _split3(a):
    hi = a.astype(BF16)
    r1 = a - hi.astype(F32)
    mid = r1.astype(BF16)
    lo = (r1 - mid.astype(F32)).astype(BF16)
    return hi, mid, lo


def _dot_nt_hi(a, b):
    m = a.shape[0]
    a_hi = a.astype(BF16)
    a_lo = (a - a_hi.astype(F32)).astype(BF16)
    b_hi = b.astype(BF16)
    b_lo = (b - b_hi.astype(F32)).astype(BF16)
    both = _dot_nt(jnp.concatenate([a_hi, a_lo], axis=0), b_hi)
    return both[0:m] + both[m:2 * m] + _dot_nt(a_hi, b_lo)


def _ada_kernel(c_ref, w_ref, b_ref, o_ref):
    cf = c_ref[...]
    ca = cf * (1.0 / (1.0 + jnp.exp(-cf)))
    w = w_ref[...]
    c0, c1, c2 = _split3(ca)
    w0, w1, w2 = _split3(w)
    out = _dot(c0, w2) + _dot(c1, w1) + _dot(c2, w0)
    out = out + _dot(c0, w1) + _dot(c1, w0)
    out = out + _dot(c0, w0)
    o_ref[...] = out + b_ref[...]


def _ada(c, w_ada, b_ada):
    depth, d, n = w_ada.shape
    bsz = c.shape[0]
    tn = 512
    return pl.pallas_call(
        _ada_kernel,
        grid=(depth, n // tn),
        in_specs=[
            pl.BlockSpec((bsz, d), lambda l, j: (0, 0)),
            pl.BlockSpec((None, d, tn), lambda l, j: (l, 0, j)),
            pl.BlockSpec((None, 1, tn), lambda l, j: (l, 0, j)),
        ],
        out_specs=pl.BlockSpec((None, bsz, tn), lambda l, j: (l, 0, j)),
        out_shape=jax.ShapeDtypeStruct((depth, bsz, n), F32),
        compiler_params=_params("arbitrary", "arbitrary"),
        name="ada",
    )(c, w_ada, b_ada.reshape(depth, 1, n))


def _gelu_tanh(x):
    return 0.5 * x * (1.0 + jnp.tanh(math.sqrt(2.0 / math.pi) * (x + 0.044715 * (x * x * x))))


def _inproj_kernel(x_ref, g_ref, sh_ref, sc_ref, w_ref, rc_ref, rs1_ref, rs2_ref, o_ref, qkv_ref):
    xf = x_ref[...]
    ms = jnp.mean(xf * xf, axis=-1, keepdims=True)
    y = xf * lax.rsqrt(ms + EPS) * g_ref[...]
    h = (y * (1.0 + sc_ref[...]) + sh_ref[...]).astype(BF16)
    n_tiles = w_ref.shape[1] // PROJ_TN
    n_gate_tiles = GATE_COLS // PROJ_TN

    def product(j):
        src = (j + (n_tiles - n_gate_tiles)) % n_tiles
        return _dot(h, w_ref[:, src * PROJ_TN:(src + 1) * PROJ_TN])

    def heads(acc, c0, rope_scale):
        rc, rs1, rs2 = rc_ref[...], rs1_ref[...], rs2_ref[...]
        for gi in range(PROJ_TN // LANE):
            t = acc[:, gi * LANE:(gi + 1) * LANE]
            if rope_scale is not None:
                t = t * rc + pltpu.roll(t, LANE - ROPE_DIM // 2, 1) * rs1 + pltpu.roll(t, ROPE_DIM // 2, 1) * rs2
                t = t * rope_scale
            qkv_ref[(c0 - WORK_Q) // LANE + gi] = t.astype(qkv_ref.dtype)

    def epilogue(acc, j):
        c0 = j * PROJ_TN
        if c0 < WORK_Q:
            ab = acc.astype(BF16)
            o_ref[:, c0:c0 + PROJ_TN] = (0.5 * jnp.tanh(0.5 * ab) + 0.5).astype(o_ref.dtype)
        elif c0 < WORK_K:
            heads(acc, c0, Q_SCALE)
        elif c0 < WORK_V:
            heads(acc, c0, 1.0)
        elif c0 < WORK_FB:
            heads(acc, c0, None)
        elif c0 < WORK_ZU:
            o_ref[:, COL_FB:COL_FB + PROJ_TN] = acc.astype(o_ref.dtype)
        else:
            dst = COL_ZU + (c0 - WORK_ZU)
            o_ref[:, dst:dst + PROJ_TN] = _gelu_tanh(acc).astype(o_ref.dtype)

    acc_next = product(0)
    for j in range(n_tiles):
        acc = acc_next
        if j + 1 < n_tiles:
            acc_next = product(j + 1)
        epilogue(acc, j)


def _inproj(x2, g, sh, sc, w_bf, layer, rc, rs1, rs2, seq, tm):
    rows, d = x2.shape
    n = w_bf.shape[2]
    per_b = seq // tm
    return pl.pallas_call(
        _inproj_kernel,
        grid=(rows // tm,),
        in_specs=[
            pl.BlockSpec((tm, d), lambda i: (i, 0)),
            pl.BlockSpec((1, d), lambda i: (0, 0)),
            pl.BlockSpec((None, 1, d), lambda i: (i // per_b, 0, 0)),
            pl.BlockSpec((None, 1, d), lambda i: (i // per_b, 0, 0)),
            pl.BlockSpec((None, d, n), lambda i: (layer, 0, 0), pipeline_mode=pl.Buffered(1)),
            pl.BlockSpec((tm, LANE), lambda i: (i, 0)),
            pl.BlockSpec((tm, LANE), lambda i: (i, 0)),
            pl.BlockSpec((tm, LANE), lambda i: (i, 0)),
        ],
        out_specs=[
            pl.BlockSpec((tm, PROJ_COLS), lambda i: (i, 0)),
            pl.BlockSpec((3 * A_HEADS, tm, LANE), lambda i: (0, i, 0)),
        ],
        out_shape=[
            jax.ShapeDtypeStruct((rows, PROJ_COLS), BF16),
            jax.ShapeDtypeStruct((3 * A_HEADS, rows, LANE), BF16),
        ],
        compiler_params=_params("arbitrary"),
        name="inproj",
    )(x2, g, sh, sc, w_bf, rc, rs1, rs2)


def _attn_kernel(lam_init, tq, kc, lq1_ref, lk1_ref, lq2_ref, lk2_ref, ng_ref, q_ref, k_ref, v_ref, o_ref, vt_scr):
    seq = k_ref.shape[0]
    n_ext = vt_scr.shape[0]
    vt_scr[0:A_V_DIM, :] = v_ref[...].astype(F32).T.astype(BF16)
    vt_scr[A_V_DIM:, :] = jnp.ones((n_ext - A_V_DIM, seq), BF16)

    lam = (jnp.exp(jnp.sum(lq1_ref[...] * lk1_ref[...], axis=-1, keepdims=True))
           - jnp.exp(jnp.sum(lq2_ref[...] * lk2_ref[...], axis=-1, keepdims=True)) + lam_init)
    out_gain = ng_ref[...] * (1.0 - lam_init)
    lane = lax.broadcasted_iota(jnp.int32, (tq, LANE), 1)
    comp_mask = [_onehot((lane >= c * A_HEAD_DIM) & (lane < (c + 1) * A_HEAD_DIM)) for c in range(2)]

    n_chunks = seq // kc
    ahead = min(2, n_chunks)

    def scores(qm, ci):
        kblk = k_ref[ci * kc:(ci + 1) * kc, :]
        return [_dot_nt(kblk, qm[c]) for c in range(2)]

    def start_tile(i):
        q = q_ref[pl.ds(pl.multiple_of(i * tq, tq), tq), :]
        qm = [q * comp_mask[c] for c in range(2)]
        return qm, [scores(qm, ci) for ci in range(ahead)]

    def finish_tile(qm, pending):
        m = [None, None]
        acc = [None, None]
        for ci in range(n_chunks):
            st_cur = pending.pop(0)
            if ci + ahead < n_chunks:
                pending.append(scores(qm, ci + ahead))
            vblk = vt_scr[:, ci * kc:(ci + 1) * kc]
            for c in range(2):
                st = st_cur[c]
                cm = jnp.max(st, axis=0, keepdims=True)
                if ci == 0:
                    m[c] = cm
                    acc[c] = _dot(vblk, jnp.exp2(st - cm).astype(BF16))
                else:
                    m_new = jnp.maximum(m[c], cm)
                    p = jnp.exp2(st - m_new).astype(BF16)
                    acc[c] = acc[c] * jnp.exp2(m[c] - m_new) + _dot(vblk, p)
                    m[c] = m_new
        return acc[0], acc[1]

    def write_tile(i, acc):
        outs = [acc[c][0:A_V_DIM, :] / acc[c][A_V_DIM:A_V_DIM + 1, :] for c in range(2)]
        o = outs[0] - lam * outs[1]
        o = o * lax.rsqrt(jnp.mean(o * o, axis=0, keepdims=True) + EPS)
        o_ref[pl.ds(pl.multiple_of(i * tq, tq), tq), :] = (o.T * out_gain).astype(o_ref.dtype)

    def q_tile(i, acc_prev):
        qm, pending = start_tile(i)
        write_tile(i - 1, acc_prev)
        return finish_tile(qm, pending)

    acc_last = lax.fori_loop(1, seq // tq, q_tile, finish_tile(*start_tile(0)))
    write_tile(seq // tq - 1, acc_last)


def _attention(qkv, lq1, lk1, lq2, lk2, ng, lam_init, bsz, seq, tq, kc):
    rows = qkv.shape[1]
    small = pl.BlockSpec((1, A_HEAD_DIM), lambda b, h: (0, 0))
    return pl.pallas_call(
        functools.partial(_attn_kernel, lam_init, tq, kc),
        grid=(bsz, A_HEADS),
        in_specs=[
            small, small, small, small,
            pl.BlockSpec((1, A_V_DIM), lambda b, h: (0, 0)),
            pl.BlockSpec((None, seq, LANE), lambda b, h: (h, b, 0)),
            pl.BlockSpec((None, seq, LANE), lambda b, h: (A_HEADS + h, b, 0)),
            pl.BlockSpec((None, seq, LANE), lambda b, h: (2 * A_HEADS + h, b, 0)),
        ],
        out_specs=pl.BlockSpec((None, seq, LANE), lambda b, h: (h, b, 0)),
        out_shape=jax.ShapeDtypeStruct((A_HEADS, rows, LANE), BF16),
        scratch_shapes=[pltpu.VMEM((A_V_DIM + BF16_SUBLANES, seq), BF16)],
        compiler_params=_params("arbitrary", "arbitrary"),
        name="diff_attn",
    )(lq1, lk1, lq2, lk2, ng, qkv, qkv, qkv)


def _dft_mats(seq):
    n = np.arange(seq, dtype=np.int64)
    ang = 2.0 * np.pi * ((n[:, None] * n[None, :]) % seq).astype(np.float64) / seq
    big = np.concatenate([np.cos(ang), -np.sin(ang)], axis=1) / math.sqrt(seq)
    c = np.arange(B_GROUP_DIM, dtype=np.int64)
    angc = 2.0 * np.pi * ((c[:, None] * c[None, :]) % B_GROUP_DIM).astype(np.float64) / B_GROUP_DIM
    small = np.concatenate([np.cos(angc), np.sin(angc)], axis=1) / math.sqrt(B_GROUP_DIM)
    return jnp.asarray(big, dtype=F32).astype(BF16), jnp.asarray(small, dtype=F32).astype(BF16)


def _fnet_kernel(tm, x_ref, wc_ref, d_ref, o_ref, y_scr):
    seq = x_ref.shape[0]
    for gi in range(B_GROUPS):
        xg = x_ref[:, gi * B_GROUP_DIM:(gi + 1) * B_GROUP_DIM]
        yz = _dot(xg, wc_ref[...])
        y_scr[0:seq, gi * B_GROUP_DIM:(gi + 1) * B_GROUP_DIM] = yz[:, :B_GROUP_DIM].astype(BF16)
        y_scr[seq:2 * seq, gi * B_GROUP_DIM:(gi + 1) * B_GROUP_DIM] = yz[:, B_GROUP_DIM:].astype(BF16)
    for r0 in range(0, seq, tm):
        o_ref[r0:r0 + tm, :] = _dot(d_ref[r0:r0 + tm, :], y_scr[...]).astype(o_ref.dtype)


def _fnet(proj, dbig, dsmall, bsz, seq, tm):
    rows = proj.shape[0]
    return pl.pallas_call(
        functools.partial(_fnet_kernel, tm),
        grid=(bsz,),
        in_specs=[
            pl.BlockSpec((seq, B_WIDTH), lambda b: (b, COL_FB // B_WIDTH)),
            pl.BlockSpec((B_GROUP_DIM, 2 * B_GROUP_DIM), lambda b: (0, 0)),
            pl.BlockSpec((seq, 2 * seq), lambda b: (0, 0), pipeline_mode=pl.Buffered(1)),
        ],
        out_specs=pl.BlockSpec((seq, B_WIDTH), lambda b: (b, 0)),
        out_shape=jax.ShapeDtypeStruct((rows, B_WIDTH), BF16),
        scratch_shapes=[pltpu.VMEM((2 * seq, B_WIDTH), BF16)],
        compiler_params=_params("arbitrary"),
        name="fnet",
    )(proj, dsmall, dbig)


def _sgu_kernel(u_ref, v_ref, ws_ref, bs_ref, o_ref):
    tm = u_ref.shape[0]
    for n in range(tm // C_CHUNK):
        r0 = n * C_CHUNK
        for gi in range(C_GROUPS):
            c0 = gi * C_GROUP_DIM
            vb = v_ref[r0:r0 + C_CHUNK, c0:c0 + C_GROUP_DIM].astype(F32)
            vg = vb * lax.rsqrt(jnp.mean(vb * vb, axis=-1, keepdims=True) + EPS)
            mixed = _dot(ws_ref[gi], vg.astype(BF16)) + bs_ref[gi]
            ub = u_ref[r0:r0 + C_CHUNK, c0:c0 + C_GROUP_DIM].astype(F32)
            o_ref[r0:r0 + C_CHUNK, c0:c0 + C_GROUP_DIM] = (ub * mixed).astype(o_ref.dtype)


def _sgu(proj, ws_bf, bs_col, tm):
    rows = proj.shape[0]
    return pl.pallas_call(
        _sgu_kernel,
        grid=(rows // tm,),
        in_specs=[
            pl.BlockSpec((tm, C_WIDTH), lambda i: (i, COL_ZU // C_WIDTH)),
            pl.BlockSpec((tm, C_WIDTH), lambda i: (i, COL_ZV // C_WIDTH)),
            pl.BlockSpec((C_GROUPS, C_CHUNK, C_CHUNK), lambda i: (0, 0, 0)),
            pl.BlockSpec((C_GROUPS, C_CHUNK, 1), lambda i: (0, 0, 0)),
        ],
        out_specs=pl.BlockSpec((tm, C_WIDTH), lambda i: (i, 0)),
        out_shape=jax.ShapeDtypeStruct((rows, C_WIDTH), BF16),
        compiler_params=_params("arbitrary"),
        name="sgu",
    )(proj, proj, ws_bf, bs_col)


MERGE_SUB_ROWS = 256


def _merge_kernel(a_ref, b_ref, c_ref, gates_ref, x_ref, pa_ref, pb_ref, pc_ref, wo_ref,
                  gt1_ref, g2_ref, sh2_ref, sc2_ref, wr_ref, x1_ref, h2_ref, aff_ref):
    tm, d = x_ref.shape
    n_sub = tm // MERGE_SUB_ROWS

    def branches(i):
        r = slice(i * MERGE_SUB_ROWS, (i + 1) * MERGE_SUB_ROWS)
        a = jnp.concatenate([a_ref[hd, r, :] for hd in range(a_ref.shape[0])], axis=1)
        return (_dot(a, pa_ref[...]), _dot(b_ref[r, :], pb_ref[...]), _dot(c_ref[r, :], pc_ref[...]))

    def finish(i, ys):
        r = slice(i * MERGE_SUB_ROWS, (i + 1) * MERGE_SUB_ROWS)
        merged = gates_ref[r, 0:d].astype(F32) * ys[0]
        merged = merged + gates_ref[r, d:2 * d].astype(F32) * ys[1]
        merged = merged + gates_ref[r, 2 * d:3 * d].astype(F32) * ys[2]
        x1 = x_ref[r, :] + gt1_ref[...] * _dot(merged.astype(BF16), wo_ref[...])
        x1_ref[r, :] = x1
        ms = jnp.mean(x1 * x1, axis=-1, keepdims=True)
        h = x1 * lax.rsqrt(ms + EPS) * g2_ref[...]
        h = h * (1.0 + sc2_ref[...]) + sh2_ref[...]
        h2_ref[r, :] = h.astype(h2_ref.dtype)
        return h

    def route(i, h):
        r = slice(i * MERGE_SUB_ROWS, (i + 1) * MERGE_SUB_ROWS)
        logits = _dot_nt_hi(wr_ref[...], h)
        mx = jnp.max(logits, axis=0, keepdims=True)
        ex = jnp.exp(logits - mx)
        aff_ref[:, r] = ex / jnp.sum(ex, axis=0, keepdims=True)

    nxt = branches(0)
    h_prev = None
    for i in range(n_sub):
        cur = nxt
        if i + 1 < n_sub:
            nxt = branches(i + 1)
        h = finish(i, cur)
        if h_prev is not None:
            route(i - 1, h_prev)
        h_prev = h
    route(n_sub - 1, h_prev)


def _merge(a_out, b_out, c_out, proj, x2, pa, pb, pc, wo, gt1, g2, sh2, sc2, wr_t, seq, tm):
    rows, d = x2.shape
    per_b = seq // tm
    ne = wr_t.shape[0]
    full = lambda shape: pl.BlockSpec(shape, lambda i: (0,) * len(shape), pipeline_mode=pl.Buffered(1))
    bvec = pl.BlockSpec((None, 1, d), lambda i: (i // per_b, 0, 0))
    return pl.pallas_call(
        _merge_kernel,
        grid=(rows // tm,),
        in_specs=[
            pl.BlockSpec((A_HEADS, tm, A_V_DIM), lambda i: (0, i, 0)),
            pl.BlockSpec((tm, B_WIDTH), lambda i: (i, 0)),
            pl.BlockSpec((tm, C_WIDTH), lambda i: (i, 0)),
            pl.BlockSpec((tm, GATE_COLS), lambda i: (i, 0)),
            pl.BlockSpec((tm, d), lambda i: (i, 0)),
            full(pa.shape), full(pb.shape), full(pc.shape), full(wo.shape),
            bvec, full((1, d)), bvec, bvec, full(wr_t.shape),
        ],
        out_specs=[
            pl.BlockSpec((tm, d), lambda i: (i, 0)),
            pl.BlockSpec((tm, d), lambda i: (i, 0)),
            pl.BlockSpec((ne, tm), lambda i: (0, i)),
        ],
        out_shape=[
            jax.ShapeDtypeStruct((rows, d), F32),
            jax.ShapeDtypeStruct((rows, d), BF16),
            jax.ShapeDtypeStruct((ne, rows), F32),
        ],
        compiler_params=_params("arbitrary"),
        name="merge",
    )(a_out, b_out, c_out, proj, x2, pa, pb, pc, wo, gt1, g2, sh2, sc2, wr_t)


def _route_kernel(cap, aff_ref, pos_ref, tri_scr):
    bsz, ne, seq = pos_ref.shape
    chunk = 256
    r = lax.broadcasted_iota(jnp.int32, (chunk, seq), 0)
    c = lax.broadcasted_iota(jnp.int32, (chunk, seq), 1)
    for r0 in range(0, seq, chunk):
        tri_scr[r0:r0 + chunk, :] = _onehot(r + r0 < c)

    aff = jnp.concatenate([aff_ref[:, b * seq:(b + 1) * seq] for b in range(bsz)], axis=0)
    one = jnp.ones(aff.shape, jnp.int32)
    zero = jnp.zeros(aff.shape, jnp.int32)
    thr_bits = jnp.zeros((aff.shape[0], 1), jnp.int32)
    for bit in range(30, -1, -1):
        cand = thr_bits | (1 << bit)
        cnt = jnp.sum(jnp.where(aff >= pltpu.bitcast(cand, F32), one, zero), axis=1, keepdims=True)
        thr_bits = jnp.where(cnt >= cap, cand, thr_bits)
    thr = pltpu.bitcast(thr_bits, F32)
    gt = aff > thr
    eq = aff == thr
    need = cap - jnp.sum(jnp.where(gt, one, zero), axis=1, keepdims=True)
    eq_before = _dot(_onehot(eq), tri_scr[...])
    sel = gt | (eq & (eq_before < need.astype(F32)))
    slot = _dot(_onehot(sel), tri_scr[...])
    pos = jnp.where(sel, slot.astype(jnp.int32), -one)
    for b in range(bsz):
        pos_ref[b] = pos[b * ne:(b + 1) * ne, :]


def _route(aff_t, bsz, seq, cap):
    ne = aff_t.shape[0]
    return pl.pallas_call(
        functools.partial(_route_kernel, cap),
        grid=(1,),
        in_specs=[pl.BlockSpec((ne, bsz * seq), lambda i: (0, 0))],
        out_specs=pl.BlockSpec((bsz, ne, seq), lambda i: (0, 0, 0)),
        out_shape=jax.ShapeDtypeStruct((bsz, ne, seq), jnp.int32),
        scratch_shapes=[pltpu.VMEM((seq, seq), BF16)],
        compiler_params=_params("arbitrary"),
        name="route",
    )(aff_t)


def _gather_kernel(cap, pos_ref, aff_ref, h_ref, xin_ref, g_ref):
    seq = h_ref.shape[0]
    slot = lax.broadcasted_iota(jnp.int32, (cap, seq), 0)
    for e in range(pos_ref.shape[0]):
        sel = pos_ref[e:e + 1, :] == slot
        xin_ref[e] = _dot(_onehot(sel), h_ref[...]).astype(xin_ref.dtype)
        g_ref[e] = jnp.sum(jnp.where(sel, aff_ref[e:e + 1, :], 0.0), axis=1, keepdims=True)


def _gather(pos, aff_t, h2, bsz, seq, cap):
    ne = pos.shape[1]
    d = h2.shape[1]
    return pl.pallas_call(
        functools.partial(_gather_kernel, cap),
        grid=(bsz,),
        in_specs=[
            pl.BlockSpec((None, ne, seq), lambda b: (b, 0, 0)),
            pl.BlockSpec((ne, seq), lambda b: (0, b)),
            pl.BlockSpec((seq, d), lambda b: (b, 0)),
        ],
        out_specs=[
            pl.BlockSpec((ne, None, cap, d), lambda b: (0, b, 0, 0)),
            pl.BlockSpec((ne, None, cap, 1), lambda b: (0, b, 0, 0)),
        ],
        out_shape=[
            jax.ShapeDtypeStruct((ne, bsz, cap, d), BF16),
            jax.ShapeDtypeStruct((ne, bsz, cap, 1), F32),
        ],
        compiler_params=_params("arbitrary"),
        name="moe_gather",
    )(pos, aff_t, h2)


EXPERT_SUB_ROWS = 512


def _expert_ff_slice(x_ref, w1_t, w3_t, w2, acc_scr, first=False):
    m = x_ref.shape[0]
    n_sub = m // EXPERT_SUB_ROWS

    def up(i):
        xs = x_ref[i * EXPERT_SUB_ROWS:(i + 1) * EXPERT_SUB_ROWS, :]
        return _dot_nt(xs, w1_t), _dot_nt(xs, w3_t)

    nxt = up(0)
    for i in range(n_sub):
        a, b = nxt
        if i + 1 < n_sub:
            nxt = up(i + 1)
        hid = (a * (0.5 * jnp.tanh(0.5 * a) + 0.5) * b).astype(BF16)
        rows = slice(i * EXPERT_SUB_ROWS, (i + 1) * EXPERT_SUB_ROWS)
        if first:
            acc_scr[rows, :] = _dot(hid, w2)
        else:
            acc_scr[rows, :] += _dot(hid, w2)


def _expert_kernel(n_main, n_pieces, x_ref, w1_ref, w3_ref, w2_ref, *rest):
    tail_refs, (g_ref, y_ref, acc_scr) = rest[:3 * n_pieces], rest[3 * n_pieces:]
    f = pl.program_id(1)

    def tail_of(k):
        pieces = [tail_refs[k * n_pieces + p][...] for p in range(n_pieces)]
        return jnp.concatenate(pieces, axis=0).astype(BF16)

    @pl.when(f == 0)
    def _():
        _expert_ff_slice(x_ref, tail_of(0), tail_of(1), tail_of(2), acc_scr, first=True)

    @pl.when(f > 0)
    def _():
        _expert_ff_slice(x_ref, w1_ref[...].astype(BF16), w3_ref[...].astype(BF16),
                         w2_ref[...].astype(BF16), acc_scr)

    @pl.when(f == n_main)
    def _():
        y_ref[...] = (acc_scr[...] * g_ref[...]).astype(y_ref.dtype)


def _experts(xin, w1, w3, w2, g, layer, tf):
    ne, m, d = xin.shape
    ff = w1.shape[3]
    n_main = ff // tf
    tail = ff - n_main * tf
    assert 0 < tail and tail % 8 == 0 and m % EXPERT_SUB_ROWS == 0
    w1_t = jnp.swapaxes(w1, 2, 3)
    w3_t = jnp.swapaxes(w3, 2, 3)
    piece = math.gcd(n_main * tf, tail)
    n_pieces = tail // piece
    last = n_main - 1

    def main_index(e, f):
        first = f == 0
        expert = jnp.where(first, jnp.maximum(e - 1, 0), e)
        block = jnp.where(first, jnp.where(e == 0, 0, last), f - 1)
        return (layer, expert, block, 0)

    main = pl.BlockSpec((None, None, tf, d), main_index)
    first_piece = n_main * tf // piece
    tail_specs = [pl.BlockSpec((None, None, piece, d), lambda e, f, p=p: (layer, e, first_piece + p, 0))
                  for p in range(n_pieces)]
    return pl.pallas_call(
        functools.partial(_expert_kernel, n_main, n_pieces),
        grid=(ne, n_main + 1),
        in_specs=[
            pl.BlockSpec((None, m, d), lambda e, f: (e, 0, 0)),
            main, main, main, *tail_specs, *tail_specs, *tail_specs,
            pl.BlockSpec((None, m, 1), lambda e, f: (e, 0, 0)),
        ],
        out_specs=pl.BlockSpec((None, m, d), lambda e, f: (e, 0, 0)),
        out_shape=jax.ShapeDtypeStruct((ne, m, d), BF16),
        scratch_shapes=[pltpu.VMEM((m, d), F32)],
        compiler_params=_params("arbitrary", "arbitrary"),
        name="experts",
    )(xin, w1_t, w3_t, w2, *([w1_t] * n_pieces), *([w3_t] * n_pieces), *([w2] * n_pieces), g)


def _scatter_kernel(cap, final, pos_ref, y_ref, x_ref, gt_ref, gf_ref, o_ref):
    tm = x_ref.shape[0]
    ne = y_ref.shape[0]
    slot = lax.broadcasted_iota(jnp.int32, (tm, cap), 1)
    pos = pos_ref[...]
    sel_t = jnp.concatenate([_onehot(pos[:, e:e + 1] == slot) for e in range(ne)], axis=1)
    acc = _dot(sel_t, y_ref[...].reshape(ne * cap, y_ref.shape[2]))
    xn = x_ref[...] + gt_ref[...] * acc
    if final:
        ms = jnp.mean(xn * xn, axis=-1, keepdims=True)
        xn = xn * lax.rsqrt(ms + EPS) * gf_ref[...]
    o_ref[...] = xn


def _scatter(pos_t, y4, x1, gt2, gf, seq, cap, tm, final):
    rows, d = x1.shape
    ne = y4.shape[0]
    per_b = seq // tm
    return pl.pallas_call(
        functools.partial(_scatter_kernel, cap, final),
        grid=(rows // tm,),
        in_specs=[
            pl.BlockSpec((tm, ne), lambda i: (i, 0)),
            pl.BlockSpec((ne, None, cap, d), lambda i: (0, i // per_b, 0, 0)),
            pl.BlockSpec((tm, d), lambda i: (i, 0)),
            pl.BlockSpec((None, 1, d), lambda i: (i // per_b, 0, 0)),
            pl.BlockSpec((1, d), lambda i: (0, 0)),
        ],
        out_specs=pl.BlockSpec((tm, d), lambda i: (i, 0)),
        out_shape=jax.ShapeDtypeStruct((rows, d), F32),
        compiler_params=_params("arbitrary"),
        name="moe_scatter",
    )(pos_t, y4, x1, gt2, gf)


def _rope_lane_tables(positions):
    half = ROPE_DIM // 2
    inv_freq = ROPE_THETA ** (-jnp.arange(0, ROPE_DIM, 2, dtype=F32) / ROPE_DIM)
    comp_dim = np.arange(LANE) % A_HEAD_DIM
    ang = positions.astype(F32).reshape(-1, 1) * inv_freq[comp_dim % half][None, :]
    cos, sin = jnp.cos(ang), jnp.sin(ang)
    first = jnp.asarray(comp_dim < half)[None, :]
    second = jnp.asarray((comp_dim >= half) & (comp_dim < ROPE_DIM))[None, :]
    rc = jnp.where(first | second, cos, 1.0)
    rs1 = jnp.where(first, -sin, 0.0)
    rs2 = jnp.where(second, sin, 0.0)
    return rc, rs1, rs2


def _lambda_init(layer_idx):
    return 0.8 - 0.6 * math.exp(-0.3 * layer_idx)


def kernel(x, c, positions, w_ada, b_ada, g_mix, g_ffn, w_in, lam_q1, lam_k1, lam_q2, lam_k2,
           a_norm_g, w_s, b_s, p_a, p_b, p_c, w_out, w_router, w1, w3, w2, g_final):
    bsz, seq, d = x.shape
    depth = w_in.shape[0]
    rows = bsz * seq
    cap = EC_FACTOR * seq // N_EXPERTS
    rc, rs1, rs2 = _rope_lane_tables(positions)
    dbig, dsmall = _dft_mats(seq)
    ada = _ada(c, w_ada, b_ada)
    x2 = x.reshape(rows, d)

    w_in_p = w_in.astype(BF16)

    for l in range(depth):
        mod = [ada[l, :, i * d:(i + 1) * d].reshape(bsz, 1, d) for i in range(6)]
        sh1, sc1, gt1, sh2, sc2, gt2 = mod

        proj, qkv = _inproj(x2, g_mix[l].reshape(1, d), sh1, sc1, w_in_p, l, rc, rs1, rs2, seq, tm=512)
        a_out = _attention(qkv, lam_q1[l].reshape(1, -1), lam_k1[l].reshape(1, -1),
                           lam_q2[l].reshape(1, -1), lam_k2[l].reshape(1, -1),
                           a_norm_g[l].reshape(1, -1), _lambda_init(l), bsz, seq, tq=1024, kc=512)
        b_out = _fnet(proj, dbig, dsmall, bsz, seq, tm=512)
        c_out = _sgu(proj, w_s[l].astype(BF16), b_s[l].reshape(C_GROUPS, C_CHUNK, 1), tm=1024)
        x1, h2, aff_t = _merge(a_out, b_out, c_out, proj, x2,
                               p_a[l].astype(BF16), p_b[l].astype(BF16), p_c[l].astype(BF16),
                               w_out[l].astype(BF16), gt1, g_ffn[l].reshape(1, d), sh2, sc2,
                               w_router[l].T, seq, tm=1024)

        pos = _route(aff_t, bsz, seq, cap)
        xin, gsel = _gather(pos, aff_t, h2, bsz, seq, cap)
        y = _experts(xin.reshape(N_EXPERTS, bsz * cap, d), w1, w3, w2,
                     gsel.reshape(N_EXPERTS, bsz * cap, 1), l, tf=512)
        pos_t = jnp.swapaxes(pos, 1, 2).reshape(rows, N_EXPERTS)
        x2 = _scatter(pos_t, y.reshape(N_EXPERTS, bsz, cap, d), x1, gt2, g_final.reshape(1, d),
                      seq, cap, tm=512, final=(l == depth - 1))

    return x2.reshape(bsz, seq, d)
```
